```python
import math
import jax, jax.numpy as jnp
from jax import lax
import numpy as np

D_MODEL = 1024
BATCH = 16
SEQ = 2048
DEPTH = 1

ATT_HEADS = 8
ATT_KV_HEADS = 2
ATT_Q_PER_KV = ATT_HEADS // ATT_KV_HEADS
ATT_HEAD_DIM = 64
WINDOW = 128
ATT_BLOCK = 128
NEG_INF = -1e30
RET_HEADS = 8
RET_DK = 64
RET_DV = 128
RET_CHUNK = 128
ATT_Q_W = ATT_HEADS * ATT_HEAD_DIM
ATT_KV_W = ATT_KV_HEADS * ATT_HEAD_DIM
RET_QK_W = RET_HEADS * RET_DK
RET_V_W = RET_HEADS * RET_DV
SPLIT_SIZES = (ATT_Q_W, ATT_KV_W, ATT_KV_W, RET_QK_W, RET_QK_W, RET_V_W, RET_V_W, D_MODEL, D_MODEL)
SPLIT_POINTS = tuple(int(v) for v in np.cumsum(SPLIT_SIZES)[:-1])
IN_W = int(sum(SPLIT_SIZES))
N_GROUPS = 4
EXPERTS_PER_GROUP = 8
N_EXPERTS = N_GROUPS * EXPERTS_PER_GROUP
TOP_K = 2
D_EXPERT = 512
MOE_BLOCK = 128
EPS = 1e-6

kernel_name = "hybrid_swa_retention_hiermoe_block"


def rms_norm(x, g):
    xf = x.astype(jnp.float32)
    y = xf * lax.rsqrt(jnp.mean(xf * xf, axis=-1, keepdims=True) + EPS)
    return (y * g.astype(jnp.float32)).astype(x.dtype)


def alibi_slopes():
    return 2.0 ** (-8.0 * jnp.arange(1, ATT_HEADS + 1, dtype=jnp.float32) / ATT_HEADS)


def sliding_window_attention(q, k, v, sinks):
    B, S = q.shape[0], q.shape[1]
    N, C = S // ATT_BLOCK, ATT_BLOCK
    qb = q.reshape(B, N, C, ATT_KV_HEADS, ATT_Q_PER_KV, ATT_HEAD_DIM).astype(jnp.float32)
    kb = k.reshape(B, N, C, ATT_KV_HEADS, ATT_HEAD_DIM).astype(jnp.float32)
    vb = v.reshape(B, N, C, ATT_KV_HEADS, ATT_HEAD_DIM).astype(jnp.float32)

    def band(t):
        prev = jnp.pad(t, ((0, 0), (1, 0), (0, 0), (0, 0), (0, 0)))[:, :-1]
        return jnp.concatenate([prev, t], axis=2)

    kband, vband = band(kb), band(vb)
    s = jnp.einsum("bnikgd,bnjkd->bnkgij", qb, kband) * (ATT_HEAD_DIM ** -0.5)
    qi = jnp.arange(C)[:, None] + C
    kj = jnp.arange(2 * C)[None, :]
    dist = qi - kj
    in_window = (dist >= 0) & (dist < WINDOW)
    blk = jnp.arange(N)[:, None, None]
    valid = in_window[None] & ((blk > 0) | (kj[None] >= C))
    slopes = alibi_slopes().reshape(ATT_KV_HEADS, ATT_Q_PER_KV, 1, 1)
    s = s - slopes * dist.astype(jnp.float32)
    s = jnp.where(valid[None, :, None, None], s, NEG_INF)
    sink = sinks.astype(jnp.float32).reshape(ATT_KV_HEADS, ATT_Q_PER_KV, 1, 1)
    m = jnp.maximum(jnp.max(s, axis=-1, keepdims=True), sink)
    p = jnp.exp(s - m)
    denom = jnp.sum(p, axis=-1, keepdims=True) + jnp.exp(sink - m)
    o = jnp.einsum("bnkgij,bnjkd->bnikgd", p / denom, vband)
    return o.reshape(B, S, ATT_Q_W).astype(q.dtype)


def retention(q, k, v):
    B, S = q.shape[0], q.shape[1]
    N, C = S // RET_CHUNK, RET_CHUNK
    log_g = jnp.log(1.0 - 2.0 ** (-5.0 - jnp.arange(RET_HEADS, dtype=jnp.float32)))
    qc = q.reshape(B, N, C, RET_HEADS, RET_DK).astype(jnp.float32)
    kc = k.reshape(B, N, C, RET_HEADS, RET_DK).astype(jnp.float32) * (RET_DK ** -0.5)
    vc = v.reshape(B, N, C, RET_HEADS, RET_DV).astype(jnp.float32)
    pos = jnp.arange(C, dtype=jnp.float32)
    diff = pos[:, None] - pos[None, :]
    decay = jnp.where(diff >= 0, jnp.exp(jnp.maximum(diff, 0.0)[None] * log_g[:, None, None]), 0.0)
    scores = jnp.einsum("bnihd,bnjhd->bnhij", qc, kc) * decay
    inner = jnp.einsum("bnhij,bnjhe->bnihe", scores, vc)
    k_dec = kc * jnp.exp((C - 1 - pos)[:, None] * log_g)[None, None, :, :, None]
    kv = jnp.einsum("bnjhd,bnjhe->nbhde", k_dec, vc)
    chunk_decay = jnp.exp(C * log_g)[None, :, None, None]

    def step(state, kv_n):
        return state * chunk_decay + kv_n, state

    _, state_prev = lax.scan(step, jnp.zeros((B, RET_HEADS, RET_DK, RET_DV), jnp.float32), kv)
    q_dec = qc * jnp.exp((pos + 1.0)[:, None] * log_g)[None, None, :, :, None]
    cross = jnp.einsum("bnihd,nbhde->bnihe", q_dec, state_prev)
    y = (inner + cross).reshape(B, S, RET_HEADS, RET_DV)
    mu = jnp.mean(y, axis=-1, keepdims=True)
    var = jnp.mean(jnp.square(y - mu), axis=-1, keepdims=True)
    return (y - mu) * lax.rsqrt(var + EPS)


def gated_mixer(h, w_in, b_in, attn_sinks, w_attn_up, w_ret_up, w_out):
    B, S = h.shape[0], h.shape[1]
    proj = h @ w_in + b_in
    aq, ak, av, rq, rk, rv, rg, ga, gr = jnp.split(proj, SPLIT_POINTS, axis=-1)
    attn = sliding_window_attention(
        aq.reshape(B, S, ATT_HEADS, ATT_HEAD_DIM),
        ak.reshape(B, S, ATT_KV_HEADS, ATT_HEAD_DIM),
        av.reshape(B, S, ATT_KV_HEADS, ATT_HEAD_DIM), attn_sinks)
    attn = attn @ w_attn_up
    ret = retention(rq.reshape(B, S, RET_HEADS, RET_DK),
                    rk.reshape(B, S, RET_HEADS, RET_DK),
                    rv.reshape(B, S, RET_HEADS, RET_DV))
    ret = (ret.reshape(B, S, RET_V_W) * jax.nn.silu(rg.astype(jnp.float32))).astype(h.dtype)
    ret = ret @ w_ret_up
    mix = jax.nn.sigmoid(ga) * attn + jax.nn.sigmoid(gr) * ret
    return mix @ w_out


def hierarchical_moe(h, w_group_router, b_group_router, w_expert_router, b_expert_router,
                     w_gate_e, w_up_e, w_down_e):
    B, S, D = h.shape
    T = B * S
    A = T * TOP_K
    P = A + N_EXPERTS * MOE_BLOCK
    NB = P // MOE_BLOCK
    ht = h.reshape(T, D)
    g_logits = (ht @ w_group_router + b_group_router).astype(jnp.float32)
    g_probs = jax.nn.softmax(g_logits, axis=-1)
    g_idx = jnp.argmax(g_logits, axis=-1)
    g_prob = jnp.take_along_axis(g_probs, g_idx[:, None], axis=-1)
    e_logits = (ht @ w_expert_router + b_expert_router).astype(jnp.float32)
    e_logits = e_logits.reshape(T, N_GROUPS, EXPERTS_PER_GROUP)
    e_in_group = jnp.take_along_axis(e_logits, g_idx[:, None, None], axis=1)[:, 0]
    top_vals, top_idx = lax.top_k(e_in_group, TOP_K)
    weights = g_prob * jax.nn.softmax(top_vals, axis=-1)
    expert_ids = (g_idx[:, None] * EXPERTS_PER_GROUP + top_idx).reshape(A).astype(jnp.int32)
    tok_ids = jnp.arange(A, dtype=jnp.int32) // TOP_K
    order = jnp.argsort(expert_ids)
    sorted_e = expert_ids[order]
    counts = jnp.zeros((N_EXPERTS,), jnp.int32).at[expert_ids].add(1)
    pad_sizes = ((counts + MOE_BLOCK - 1) // MOE_BLOCK) * MOE_BLOCK
    pad_end = jnp.cumsum(pad_sizes)
    pad_start = pad_end - pad_sizes
    cnt_start = jnp.cumsum(counts) - counts
    dest_sorted = pad_start[sorted_e] + (jnp.arange(A, dtype=jnp.int32) - cnt_start[sorted_e])
    dest = jnp.zeros((A,), jnp.int32).at[order].set(dest_sorted)
    buf_tok = jnp.full((P,), T, jnp.int32).at[dest].set(tok_ids)
    h_pad = jnp.concatenate([ht, jnp.zeros((1, D), ht.dtype)], axis=0)
    x_blocks = h_pad[buf_tok].reshape(NB, MOE_BLOCK, D)
    block_start = jnp.arange(NB, dtype=jnp.int32) * MOE_BLOCK
    block_e = jnp.minimum(jnp.searchsorted(pad_end, block_start, side="right"), N_EXPERTS - 1)

    def expert_block(args):
        xb, e = args
        return (jax.nn.silu(xb @ w_gate_e[e]) * (xb @ w_up_e[e])) @ w_down_e[e]

    y_buf = lax.map(expert_block, (x_blocks, block_e)).reshape(P, D)
    y = jnp.sum(y_buf[dest].reshape(T, TOP_K, D) * weights[..., None].astype(h.dtype), axis=1)
    return y.reshape(B, S, D)


def setup_inputs(seed: int = 0) -> dict:
    key = jax.random.key(seed)
    ks = jax.random.split(key, 20)
    L, D = DEPTH, D_MODEL
    f32 = jnp.float32

    def nrm(k, shape, scale):
        return jax.random.normal(k, shape, f32) * scale

    return {
        "x": jax.random.normal(ks[0], (BATCH, SEQ, D), f32),
        "norm_mix_g": 1.0 + nrm(ks[1], (L, D), 0.02),
        "w_in": nrm(ks[2], (L, D, IN_W), D ** -0.5),
        "b_in": nrm(ks[3], (L, IN_W), 0.02),
        "attn_sinks": nrm(ks[4], (L, ATT_HEADS), 0.5),
        "w_attn_up": nrm(ks[5], (L, ATT_Q_W, D), ATT_Q_W ** -0.5),
        "w_ret_up": nrm(ks[6], (L, RET_V_W, D), RET_V_W ** -0.5),
        "w_out": nrm(ks[7], (L, D, D), D ** -0.5),
        "norm_ffn_g": 1.0 + nrm(ks[8], (L, D), 0.02),
        "w_group_router": nrm(ks[9], (L, D, N_GROUPS), D ** -0.5),
        "b_group_router": nrm(ks[10], (L, N_GROUPS), 0.01),
        "w_expert_router": nrm(ks[11], (L, D, N_EXPERTS), D ** -0.5),
        "b_expert_router": nrm(ks[12], (L, N_EXPERTS), 0.01),
        "w_gate_e": nrm(ks[13], (L, N_EXPERTS, D, D_EXPERT), D ** -0.5),
        "w_up_e": nrm(ks[14], (L, N_EXPERTS, D, D_EXPERT), D ** -0.5),
        "w_down_e": nrm(ks[15], (L, N_EXPERTS, D_EXPERT, D), D_EXPERT ** -0.5),
        "norm_final_g": 1.0 + nrm(ks[16], (D,), 0.02),
    }


def reference(x, norm_mix_g, w_in, b_in, attn_sinks, w_attn_up, w_ret_up, w_out,
              norm_ffn_g, w_group_router, b_group_router, w_expert_router, b_expert_router,
              w_gate_e, w_up_e, w_down_e, norm_final_g):
    for l in range(DEPTH):
        h = rms_norm(x, norm_mix_g[l])
        x = x + gated_mixer(h, w_in[l], b_in[l], attn_sinks[l], w_attn_up[l], w_ret_up[l], w_out[l])
        h = rms_norm(x, norm_ffn_g[l])
        x = x + hierarchical_moe(h, w_group_router[l], b_group_router[l], w_expert_router[l],
                                 b_expert_router[l], w_gate_e[l], w_up_e[l], w_down_e[l])
    return rms_norm(x, norm_final_g)
```

```python
import math

import jax
import jax.numpy as jnp
from jax import lax
from jax.experimental import pallas as pl
from jax.experimental.pallas import tpu as pltpu

F32 = jnp.float32
BF16 = jnp.bfloat16
I32 = jnp.int32

D_MODEL = 1024
ATT_HEADS = 8
ATT_KV_HEADS = 2
ATT_HEAD_DIM = 64
WINDOW = 128
NEG_INF = -1e30
RET_HEADS = 8
RET_DK = 64
RET_DV = 128
CHUNK = 128
N_GROUPS = 4
EXPERTS_PER_GROUP = 8
N_EXPERTS = 32
D_EXPERT = 512
EPS = 1e-6

C_AQ, C_AK, C_AV, C_RQ, C_RK, C_RV, C_RG, C_GA, C_GR, C_END = 0, 512, 640, 768, 1280, 1792, 2816, 3840, 4864, 5888

LANES = 128
TM = 512
TQ = 512
TT = 512
BLK = 256
LOG_BLK = 8
ROUTER_W = 128
VMEM_LIMIT = 56 * 1024 * 1024


def _sigmoid(v):
    return 1.0 / (1.0 + jnp.exp(-v))


def _rms(v, g):
    return v * lax.rsqrt(jnp.mean(v * v, axis=-1, keepdims=True) + EPS) * g


def _dot(a, b):
    return jnp.dot(a, b, preferred_element_type=F32)


def _inproj_body(x_ref, g_ref, w_ref, b_ref, wkt_ref, bkt_ref,
                 aq_ref, akt_ref, av_ref, rq_ref, rkt_ref, rv_ref, rg_ref, ga_ref, gr_ref):
    h = _rms(x_ref[...], g_ref[...]).astype(BF16)

    def proj(lo, hi):
        return (_dot(h, w_ref[:, lo:hi]) + b_ref[:, lo:hi]).astype(BF16)

    aq_ref[...] = proj(C_AQ, C_AK)
    av_ref[...] = proj(C_AV, C_RQ)
    rq_ref[...] = proj(C_RQ, C_RK)
    rv_ref[...] = proj(C_RV, C_RG)
    rg_ref[...] = proj(C_RG, C_GA)
    ga_ref[...] = proj(C_GA, C_GR)
    gr_ref[...] = proj(C_GR, C_END)
    kt = lax.dot_general(wkt_ref[...], h, (((1,), (1,)), ((), ())), preferred_element_type=F32) + bkt_ref[...]
    akt_ref[...] = kt[0:128].astype(BF16)
    rkt_ref[...] = kt[128:640].astype(BF16)


def _inproj(x2d, g, w_bf, b, wkt_bf, bkt):
    t = x2d.shape[0]
    const = lambda i: (0, 0)
    row = lambda i: (i, 0)
    col = lambda i: (0, i)
    out_shapes = (
        jax.ShapeDtypeStruct((t, 512), BF16),
        jax.ShapeDtypeStruct((128, t), BF16),
        jax.ShapeDtypeStruct((t, 128), BF16),
        jax.ShapeDtypeStruct((t, 512), BF16),
        jax.ShapeDtypeStruct((512, t), BF16),
        jax.ShapeDtypeStruct((t, 1024), BF16),
        jax.ShapeDtypeStruct((t, 1024), BF16),
        jax.ShapeDtypeStruct((t, 1024), BF16),
        jax.ShapeDtypeStruct((t, 1024), BF16),
    )
    out_specs = (
        pl.BlockSpec((TM, 512), row), pl.BlockSpec((128, TM), col), pl.BlockSpec((TM, 128), row),
        pl.BlockSpec((TM, 512), row), pl.BlockSpec((512, TM), col), pl.BlockSpec((TM, 1024), row),
        pl.BlockSpec((TM, 1024), row), pl.BlockSpec((TM, 1024), row), pl.BlockSpec((TM, 1024), row),
    )
    return pl.pallas_call(
        _inproj_body, name="inproj", grid=(t // TM,),
        in_specs=[
            pl.BlockSpec((TM, D_MODEL), row),
            pl.BlockSpec((1, D_MODEL), const),
            pl.BlockSpec((D_MODEL, C_END), const, pipeline_mode=pl.Buffered(1)),
            pl.BlockSpec((1, C_END), const),
            pl.BlockSpec((640, D_MODEL), const, pipeline_mode=pl.Buffered(1)),
            pl.BlockSpec((640, 1), const),
        ],
        out_specs=out_specs, out_shape=out_shapes,
        compiler_params=pltpu.CompilerParams(dimension_semantics=("arbitrary",), vmem_limit_bytes=VMEM_LIMIT),
    )(x2d, g, w_bf, b, wkt_bf, bkt)


def _attn_body(sink_ref, aq_ref, ktc_ref, ktp_ref, vc_ref, vp_ref, out_ref):
    j = pl.program_id(1)
    kt_all = jnp.concatenate([ktp_ref[...], ktc_ref[...]], axis=1)
    v_all = jnp.concatenate([vp_ref[...], vc_ref[...]], axis=0)
    qi = lax.broadcasted_iota(I32, (CHUNK, 2 * CHUNK), 0)
    kj = lax.broadcasted_iota(I32, (CHUNK, 2 * CHUNK), 1)
    dist = qi + CHUNK - kj
    in_window = (dist >= 0) & (dist < WINDOW)
    distf = dist.astype(F32)
    row_lo = lax.broadcasted_iota(I32, (CHUNK, 2 * CHUNK), 0) < ATT_HEAD_DIM
    lane_lo = lax.broadcasted_iota(I32, (CHUNK, LANES), 1) < ATT_HEAD_DIM
    for i in range(TQ // CHUNK):
        r0 = i * CHUNK
        ktb = kt_all[:, r0:r0 + 2 * CHUNK]
        vb = v_all[r0:r0 + 2 * CHUNK, :]
        valid = (in_window & ((kj >= CHUNK) | (j > 0))) if i == 0 else in_window
        swapped = jnp.concatenate([ktb[ATT_HEAD_DIM:], ktb[:ATT_HEAD_DIM]], axis=0)
        zero = jnp.zeros_like(ktb)
        for k in range(ATT_KV_HEADS):
            src_lo, src_hi = (ktb, swapped) if k == 0 else (swapped, ktb)
            rhs = jnp.concatenate([jnp.where(row_lo, src_lo, zero), jnp.where(row_lo, zero, src_hi)], axis=1)
            for m in range(2):
                c0 = (2 * k + m) * LANES
                q2 = aq_ref[r0:r0 + CHUNK, c0:c0 + LANES]
                s2 = _dot(q2, rhs)
                outs = []
                for par in range(2):
                    h = 4 * k + 2 * m + par
                    slope = 2.0 ** (-(h + 1))
                    sink = sink_ref[h]
                    s = s2[:, par * 2 * CHUNK:(par + 1) * 2 * CHUNK] * (ATT_HEAD_DIM ** -0.5) - slope * distf
                    s = jnp.where(valid, s, NEG_INF)
                    mx = jnp.maximum(jnp.max(s, axis=1, keepdims=True), sink)
                    p = jnp.exp(s - mx)
                    den = jnp.sum(p, axis=1, keepdims=True) + jnp.exp(sink - mx)
                    outs.append(_dot(p.astype(BF16), vb) / den)
                oe, oo = outs
                if k == 0:
                    o2 = jnp.where(lane_lo, oe, pltpu.roll(oo, ATT_HEAD_DIM, axis=1))
                else:
                    o2 = jnp.where(lane_lo, pltpu.roll(oe, ATT_HEAD_DIM, axis=1), oo)
                out_ref[r0:r0 + CHUNK, c0:c0 + LANES] = o2.astype(BF16)


def _attention(sinks, aq, akt, av, nb, s):
    t = aq.shape[0]
    tiles = s // TQ
    cpt = TQ // CHUNK
    cur_row = lambda b, j: (b * tiles + j, 0)
    cur_col = lambda b, j: (0, b * tiles + j)
    prev_row = lambda b, j: (jnp.maximum((b * tiles + j) * cpt - 1, 0), 0)
    prev_col = lambda b, j: (0, jnp.maximum((b * tiles + j) * cpt - 1, 0))
    return pl.pallas_call(
        _attn_body, name="attn", grid=(nb, tiles),
        in_specs=[
            pl.BlockSpec(memory_space=pltpu.SMEM),
            pl.BlockSpec((TQ, 512), cur_row),
            pl.BlockSpec((128, TQ), cur_col),
            pl.BlockSpec((128, CHUNK), prev_col),
            pl.BlockSpec((TQ, 128), cur_row),
            pl.BlockSpec((CHUNK, 128), prev_row),
        ],
        out_specs=pl.BlockSpec((TQ, 512), cur_row),
        out_shape=jax.ShapeDtypeStruct((t, 512), BF16),
        compiler_params=pltpu.CompilerParams(dimension_semantics=("arbitrary", "arbitrary"),
                                             vmem_limit_bytes=VMEM_LIMIT),
    )(sinks, aq, akt, akt, av, av)


def _ret_body(rq_ref, rkt_ref, rv_ref, rg_ref, dec_ref, kdec_ref, qdec_ref, cd_ref, out_ref, state_ref):
    @pl.when(pl.program_id(1) == 0)
    def _():
        state_ref[...] = jnp.zeros_like(state_ref)

    row_lo = lax.broadcasted_iota(I32, (CHUNK, CHUNK), 0) < RET_DK
    quad = (lax.broadcasted_iota(I32, (CHUNK, 2 * RET_DV), 0) < RET_DK) == \
           (lax.broadcasted_iota(I32, (CHUNK, 2 * RET_DV), 1) < RET_DV)
    for c in range(TQ // CHUNK):
        r0 = c * CHUNK
        for m in range(RET_HEADS // 2):
            q2 = rq_ref[r0:r0 + CHUNK, m * LANES:(m + 1) * LANES]
            kt2 = rkt_ref[m * LANES:(m + 1) * LANES, r0:r0 + CHUNK]
            v2 = rv_ref[r0:r0 + CHUNK, m * 2 * RET_DV:(m + 1) * 2 * RET_DV]
            zero = jnp.zeros_like(kt2)
            rhs = jnp.concatenate([jnp.where(row_lo, kt2, zero), jnp.where(row_lo, zero, kt2)], axis=1)
            p2 = (_dot(q2, rhs) * dec_ref[m]).astype(BF16)
            inner = jnp.concatenate([_dot(p2[:, :CHUNK], v2[:, :RET_DV]), _dot(p2[:, CHUNK:], v2[:, RET_DV:])], axis=1)
            st = state_ref[m]
            y2 = inner + _dot(q2, st.astype(BF16)) * qdec_ref[m]
            kd = (kt2.astype(F32) * kdec_ref[m]).astype(BF16)
            state_ref[m] = st * cd_ref[m] + jnp.where(quad, _dot(kd, v2), 0.0)
            for par in range(2):
                h = 2 * m + par
                y = y2[:, par * RET_DV:(par + 1) * RET_DV]
                d = y - jnp.mean(y, axis=1, keepdims=True)
                yn = d * lax.rsqrt(jnp.mean(d * d, axis=1, keepdims=True) + EPS)
                g = rg_ref[r0:r0 + CHUNK, h * RET_DV:(h + 1) * RET_DV].astype(F32)
                out_ref[r0:r0 + CHUNK, h * RET_DV:(h + 1) * RET_DV] = (yn * (g * _sigmoid(g))).astype(BF16)


def _retention_consts():
    log_g = jnp.log(1.0 - 2.0 ** (-5.0 - jnp.arange(RET_HEADS, dtype=F32)))
    pos = jnp.arange(CHUNK, dtype=F32)
    diff = pos[:, None] - pos[None, :]
    decay = jnp.where(diff >= 0, jnp.exp(jnp.maximum(diff, 0.0)[None] * log_g[:, None, None]), 0.0)
    decay = decay * (RET_DK ** -0.5)
    pair = lambda a: jnp.concatenate([a[0::2], a[1::2]], axis=-1)
    dec2 = pair(decay)
    kdec = jnp.exp((CHUNK - 1 - pos)[None, :] * log_g[:, None]) * (RET_DK ** -0.5)
    kdec_rows = jnp.repeat(kdec.reshape(RET_HEADS // 2, 2, 1, CHUNK), RET_DK, axis=2)
    kdec2 = kdec_rows.reshape(RET_HEADS // 2, 2 * RET_DK, CHUNK)
    qdec = jnp.exp((pos + 1.0)[None, :] * log_g[:, None])
    qdec2 = pair(jnp.broadcast_to(qdec[:, :, None], (RET_HEADS, CHUNK, RET_DV)))
    cd = jnp.exp(CHUNK * log_g)
    cd2 = pair(jnp.broadcast_to(cd[:, None, None], (RET_HEADS, 1, RET_DV)))
    return dec2, kdec2, qdec2, cd2


def _retention(rq, rkt, rv, rg, nb, s):
    t = rq.shape[0]
    tiles = s // TQ
    dec2, kdec2, qdec2, cd2 = _retention_consts()
    cur_row = lambda b, j: (b * tiles + j, 0)
    cur_col = lambda b, j: (0, b * tiles + j)
    const3 = lambda b, j: (0, 0, 0)
    return pl.pallas_call(
        _ret_body, name="ret", grid=(nb, tiles),
        in_specs=[
            pl.BlockSpec((TQ, 512), cur_row),
            pl.BlockSpec((512, TQ), cur_col),
            pl.BlockSpec((TQ, 1024), cur_row),
            pl.BlockSpec((TQ, 1024), cur_row),
            pl.BlockSpec(dec2.shape, const3),
            pl.BlockSpec(kdec2.shape, const3),
            pl.BlockSpec(qdec2.shape, const3),
            pl.BlockSpec(cd2.shape, const3),
        ],
        out_specs=pl.BlockSpec((TQ, 1024), cur_row),
        out_shape=jax.ShapeDtypeStruct((t, 1024), BF16),
        scratch_shapes=[pltpu.VMEM((RET_HEADS // 2, CHUNK, 2 * RET_DV), F32)],
        compiler_params=pltpu.CompilerParams(dimension_semantics=("arbitrary", "arbitrary"),
                                             vmem_limit_bytes=VMEM_LIMIT),
    )(rq, rkt, rv, rg, dec2, kdec2, qdec2, cd2)


def _mix_body(attn_ref, ret_ref, ga_ref, gr_ref, x_ref, wau_ref, wru_ref, wo_ref, g2_ref, wr_ref, br_ref,
              x1_ref, lg_ref):
    a = _dot(attn_ref[...], wau_ref[...])
    r = _dot(ret_ref[...], wru_ref[...])
    mix = _sigmoid(ga_ref[...].astype(F32)) * a + _sigmoid(gr_ref[...].astype(F32)) * r
    x1 = x_ref[...] + _dot(mix.astype(BF16), wo_ref[...])
    x1_ref[...] = x1
    h2 = _rms(x1, g2_ref[...]).astype(BF16)
    lg_ref[...] = _dot(h2, wr_ref[...]) + br_ref[...]


def _mix(attn, ret, ga, gr, x2d, wau, wru, wo, g2, wr, br):
    t = x2d.shape[0]
    row = lambda i: (i, 0)
    const = lambda i: (0, 0)
    return pl.pallas_call(
        _mix_body, name="mix", grid=(t // TM,),
        in_specs=[
            pl.BlockSpec((TM, 512), row), pl.BlockSpec((TM, 1024), row), pl.BlockSpec((TM, 1024), row),
            pl.BlockSpec((TM, 1024), row), pl.BlockSpec((TM, D_MODEL), row),
            pl.BlockSpec((512, D_MODEL), const), pl.BlockSpec((1024, D_MODEL), const),
            pl.BlockSpec((D_MODEL, D_MODEL), const), pl.BlockSpec((1, D_MODEL), const),
            pl.BlockSpec((D_MODEL, ROUTER_W), const), pl.BlockSpec((1, ROUTER_W), const),
        ],
        out_specs=(pl.BlockSpec((TM, D_MODEL), row), pl.BlockSpec((TM, ROUTER_W), row)),
        out_shape=(jax.ShapeDtypeStruct((t, D_MODEL), F32), jax.ShapeDtypeStruct((t, ROUTER_W), F32)),
        compiler_params=pltpu.CompilerParams(dimension_semantics=("arbitrary",), vmem_limit_bytes=VMEM_LIMIT),
    )(attn, ret, ga, gr, x2d, wau, wru, wo, g2, wr, br)


def _route_body(lg_ref, tri_ref, info_ref, w_ref, cnt_ref, carry_ref):
    @pl.when(pl.program_id(0) == 0)
    def _():
        carry_ref[...] = jnp.zeros_like(carry_ref)

    lt = lg_ref[...].T
    best = lt[0:1]
    gi = jnp.zeros((1, TT), I32)
    for r in range(1, N_GROUPS):
        better = lt[r:r + 1] > best
        best = jnp.where(better, lt[r:r + 1], best)
        gi = jnp.where(better, r, gi)
    gsum = jnp.zeros((1, TT), F32)
    for r in range(N_GROUPS):
        gsum = gsum + jnp.exp(lt[r:r + 1] - best)
    gp = 1.0 / gsum
    eg = lt[8:16]
    for r in range(1, N_GROUPS):
        eg = jnp.where(gi == r, lt[8 + 8 * r:16 + 8 * r], eg)
    r8 = lax.broadcasted_iota(I32, (EXPERTS_PER_GROUP, TT), 0).astype(F32)
    none = float(EXPERTS_PER_GROUP)
    m1 = jnp.max(eg, axis=0, keepdims=True)
    i1 = jnp.min(jnp.where(eg == m1, r8, none), axis=0, keepdims=True)
    eg2 = jnp.where(r8 == i1, -jnp.inf, eg)
    m2 = jnp.max(eg2, axis=0, keepdims=True)
    i2 = jnp.min(jnp.where(eg2 == m2, r8, none), axis=0, keepdims=True)
    ex = jnp.exp(m2 - m1)
    w1 = gp * (1.0 / (1.0 + ex))
    w2 = gp * (ex / (1.0 + ex))
    e1 = gi * EXPERTS_PER_GROUP + i1.astype(I32)
    e2 = gi * EXPERTS_PER_GROUP + i2.astype(I32)
    r32 = lax.broadcasted_iota(I32, (N_EXPERTS, TT), 0)
    oh1 = r32 == e1
    oh2 = r32 == e2
    used = jnp.where(oh1 | oh2, 1.0, 0.0)
    before = _dot(used.astype(BF16), tri_ref[...]) + carry_ref[:, 0:1]
    rank1 = jnp.sum(jnp.where(oh1, before, 0.0), axis=0, keepdims=True)
    rank2 = jnp.sum(jnp.where(oh2, before, 0.0), axis=0, keepdims=True)
    carry = carry_ref[...] + jnp.sum(used, axis=1, keepdims=True)
    carry_ref[...] = carry
    cnt_ref[...] = carry.astype(I32)
    info_ref[...] = jnp.zeros_like(info_ref)
    info_ref[0:1, :] = e1
    info_ref[1:2, :] = e2
    info_ref[2:3, :] = rank1.astype(I32)
    info_ref[3:4, :] = rank2.astype(I32)
    r128 = lax.broadcasted_iota(I32, (LANES, TT), 0)
    wt = jnp.where(r128 == 0, w1, jnp.where(r128 == 1, w2, 0.0))
    w_ref[...] = wt.T


def _route(logits):
    t = logits.shape[0]
    tri = jnp.triu(jnp.ones((TT, TT), BF16), 1)
    return pl.pallas_call(
        _route_body, name="route", grid=(t // TT,),
        in_specs=[pl.BlockSpec((TT, ROUTER_W), lambda i: (i, 0)), pl.BlockSpec((TT, TT), lambda i: (0, 0))],
        out_specs=(pl.BlockSpec((8, TT), lambda i: (0, i)), pl.BlockSpec((TT, LANES), lambda i: (i, 0)),
                   pl.BlockSpec((N_EXPERTS, LANES), lambda i: (0, 0))),
        out_shape=(jax.ShapeDtypeStruct((8, t), I32), jax.ShapeDtypeStruct((t, LANES), F32),
                   jax.ShapeDtypeStruct((N_EXPERTS, LANES), I32)),
        scratch_shapes=[pltpu.VMEM((N_EXPERTS, LANES), F32)],
        compiler_params=pltpu.CompilerParams(dimension_semantics=("arbitrary",)),
    )(logits, tri)


def _dest_body(info_ref, cnt_ref, ltri_ref, dest_ref, pst_ref, be_ref, nblk_ref):
    cnt = cnt_ref[...]
    nbe = (cnt + (BLK - 1)) >> LOG_BLK
    cum = _dot(ltri_ref[...], nbe.astype(F32).astype(BF16)).astype(I32)
    pst = cum << LOG_BLK
    pst_ref[...] = pst
    pend = cum + nbe
    nblkp = be_ref.shape[1]
    bidx = lax.broadcasted_iota(I32, (N_EXPERTS, nblkp), 1)
    be = jnp.sum(jnp.where(pend[:, 0:1] <= bidx, 1.0, 0.0), axis=0, keepdims=True).astype(I32)
    be_ref[...] = jnp.broadcast_to(jnp.minimum(be, N_EXPERTS - 1), be_ref.shape)
    nblk_ref[...] = jnp.broadcast_to(pend[N_EXPERTS - 1:N_EXPERTS, :], nblk_ref.shape)
    r32 = lax.broadcasted_iota(I32, (N_EXPERTS, TT), 0)
    pst_f = pst[:, 0:1].astype(F32)
    dest_ref[...] = jnp.zeros_like(dest_ref)
    for k in range(2):
        start = jnp.sum(jnp.where(r32 == info_ref[k:k + 1, :], pst_f, 0.0), axis=0, keepdims=True)
        dest_ref[k:k + 1, :] = start.astype(I32) + info_ref[2 + k:3 + k, :]


def _dest(info, cnt, nblkp):
    t = info.shape[1]
    ltri = jnp.tril(jnp.ones((N_EXPERTS, N_EXPERTS), BF16), -1)
    const = lambda i: (0, 0)
    return pl.pallas_call(
        _dest_body, name="dest", grid=(t // TT,),
        in_specs=[pl.BlockSpec((8, TT), lambda i: (0, i)), pl.BlockSpec((N_EXPERTS, LANES), const),
                  pl.BlockSpec((N_EXPERTS, N_EXPERTS), const)],
        out_specs=(pl.BlockSpec((8, TT), lambda i: (0, i)), pl.BlockSpec((N_EXPERTS, LANES), const),
                   pl.BlockSpec((8, nblkp), const), pl.BlockSpec((8, LANES), const)),
        out_shape=(jax.ShapeDtypeStruct((8, t), I32), jax.ShapeDtypeStruct((N_EXPERTS, LANES), I32),
                   jax.ShapeDtypeStruct((8, nblkp), I32), jax.ShapeDtypeStruct((8, LANES), I32)),
        compiler_params=pltpu.CompilerParams(dimension_semantics=("arbitrary",)),
    )(info, cnt, ltri)


def _perm_body(dest_ref, cnt_ref, pst_ref, slot_ref, *, n_tok):
    n_asg = 2 * n_tok

    def per_expert(e, n_pad):
        c = cnt_ref[e, 0]
        st = pst_ref[e, 0]
        hi = st + (((c + (BLK - 1)) >> LOG_BLK) << LOG_BLK)

        def fill(s, n):
            slot_ref[s >> 7, s & 127] = n_asg + n
            return n + 1

        return lax.fori_loop(st + c, hi, fill, n_pad)

    lax.fori_loop(0, N_EXPERTS, per_expert, 0)

    rows_per_k = n_tok // LANES

    unroll = 8

    def per_group(i, carry):
        r = i >> 4
        c0 = (i & 15) * unroll
        k = jnp.where(r >= rows_per_k, 1, 0)
        a0 = 2 * ((r - k * rows_per_k) * LANES + c0) + k
        for c in range(unroll):
            d = dest_ref[r, c0 + c]
            slot_ref[d >> 7, d & 127] = a0 + 2 * c
        return carry

    lax.fori_loop(0, 2 * rows_per_k * (LANES // unroll), per_group, 0)


def _perm(dest2d, cnt, pst, n_tok, n_slots):
    import functools
    smem = pl.BlockSpec(memory_space=pltpu.SMEM)
    return pl.pallas_call(
        functools.partial(_perm_body, n_tok=n_tok), name="perm",
        in_specs=[smem, smem, smem], out_specs=smem,
        out_shape=jax.ShapeDtypeStruct((n_slots // LANES, LANES), I32),
    )(dest2d, cnt, pst)


SUB = 8


def _expert_body(be_ref, nb_ref, slot_hbm, x1_hbm, g2_ref, wg_ref, wu_ref, wd_ref, rows_hbm,
                 idx_smem, xbuf, ybuf, wg_bf, wu_bf, wd_bf, sem_idx, sem_x, sem_y, *, n_asg):
    b = pl.program_id(0)
    nb = nb_ref[0]

    def idx_copy(blk, s):
        return pltpu.make_async_copy(slot_hbm.at[blk], idx_smem.at[s], sem_idx.at[s])

    def gather_wait(s):
        pltpu.make_async_copy(x1_hbm.at[pl.ds(0, BLK * SUB)], xbuf.at[s], sem_x.at[s]).wait()

    def scatter_wait(s):
        pltpu.make_async_copy(ybuf.at[s], rows_hbm.at[pl.ds(0, BLK * SUB)], sem_y.at[s]).wait()

    def tile_rows(i):
        return pl.ds(pl.multiple_of(i * SUB, SUB), SUB)

    def issue_gather(si, sx):
        def grp(q, carry):
            for u in range(8):
                r = q * 8 + u
                a = idx_smem[si, r >> 7, r & 127]
                tok = jnp.where(a < n_asg, a >> 1, 0)
                pltpu.make_async_copy(x1_hbm.at[tile_rows(tok)], xbuf.at[sx, tile_rows(r)], sem_x.at[sx]).start()
            return carry
        lax.fori_loop(0, BLK // 8, grp, 0)

    def issue_scatter(si, sy):
        def grp(q, carry):
            for u in range(8):
                r = q * 8 + u
                a = idx_smem[si, r >> 7, r & 127]
                pltpu.make_async_copy(ybuf.at[sy, tile_rows(r)], rows_hbm.at[tile_rows(a)], sem_y.at[sy]).start()
            return carry
        lax.fori_loop(0, BLK // 8, grp, 0)

    @pl.when(b == 0)
    def _():
        first = idx_copy(0, 0)
        first.start()
        first.wait()
        issue_gather(0, 0)

        @pl.when(nb > 1)
        def _():
            idx_copy(1, 1).start()

    @pl.when(b < nb)
    def _():
        s2 = b % 2
        s3 = b % 3

        @pl.when(b + 1 < nb)
        def _():
            idx_copy(b + 1, (b + 1) % 3).wait()
            issue_gather((b + 1) % 3, 1 - s2)

        @pl.when(b + 2 < nb)
        def _():
            idx_copy(b + 2, (b + 2) % 3).start()

        @pl.when((b == 0) | (be_ref[b] != be_ref[jnp.maximum(b - 1, 0)]))
        def _():
            wg_bf[...] = wg_ref[0].astype(BF16)
            wu_bf[...] = wu_ref[0].astype(BF16)
            wd_bf[...] = wd_ref[0].astype(BF16)

        gather_wait(s2)

        @pl.when(b >= 2)
        def _():
            scatter_wait(s2)

        x = jnp.concatenate([xbuf[s2, pl.ds(u, BLK, stride=SUB), :] for u in range(SUB)], axis=-1)
        h = _rms(x, g2_ref[...]).astype(BF16)
        g = _dot(h, wg_bf[...])
        u_ = _dot(h, wu_bf[...])
        act = (g * _sigmoid(g) * u_).astype(BF16)
        y = _dot(act, wd_bf[...])
        for u in range(SUB):
            ybuf[s2, pl.ds(u, BLK, stride=SUB), :] = y[:, u * LANES:(u + 1) * LANES]
        issue_scatter(s3, s2)

        @pl.when(b == nb - 1)
        def _():
            @pl.when(b >= 1)
            def _():
                scatter_wait(1 - s2)
            scatter_wait(s2)


def _experts(block_e, nblk, slot3d, x1_3d, g2, wg, wu, wd, n_rows):
    import functools
    nblocks = slot3d.shape[0]
    n_asg = 2 * (x1_3d.shape[0] // SUB)
    wmap = lambda b, be, nb: (be[b], 0, 0)
    grid_spec = pltpu.PrefetchScalarGridSpec(
        num_scalar_prefetch=2, grid=(nblocks,),
        in_specs=[
            pl.BlockSpec(memory_space=pl.ANY),
            pl.BlockSpec(memory_space=pl.ANY),
            pl.BlockSpec((1, D_MODEL), lambda b, be, nb: (0, 0)),
            pl.BlockSpec((1, D_MODEL, D_EXPERT), wmap),
            pl.BlockSpec((1, D_MODEL, D_EXPERT), wmap),
            pl.BlockSpec((1, D_EXPERT, D_MODEL), wmap),
        ],
        out_specs=pl.BlockSpec(memory_space=pl.ANY),
        scratch_shapes=[
            pltpu.SMEM((3, BLK // LANES, LANES), I32),
            pltpu.VMEM((2, BLK * SUB, LANES), F32),
            pltpu.VMEM((2, BLK * SUB, LANES), F32),
            pltpu.VMEM((D_MODEL, D_EXPERT), BF16),
            pltpu.VMEM((D_MODEL, D_EXPERT), BF16),
            pltpu.VMEM((D_EXPERT, D_MODEL), BF16),
            pltpu.SemaphoreType.DMA((3,)),
            pltpu.SemaphoreType.DMA((2,)),
            pltpu.SemaphoreType.DMA((2,)),
        ],
    )
    return pl.pallas_call(
        functools.partial(_expert_body, n_asg=n_asg), name="experts", grid_spec=grid_spec,
        out_shape=jax.ShapeDtypeStruct((n_rows * SUB, LANES), F32),
        compiler_params=pltpu.CompilerParams(dimension_semantics=("arbitrary",), vmem_limit_bytes=VMEM_LIMIT),
    )(block_e, nblk, slot3d, x1_3d, g2, wg, wu, wd)


def _final_body(x1_ref, rows_ref, w_ref, g_ref, out_ref):
    r = rows_ref[...]
    w = w_ref[...]
    x2 = x1_ref[...] + (r[:, :D_MODEL] * w[:, 0:1] + r[:, D_MODEL:] * w[:, 1:2])
    out_ref[...] = _rms(x2, g_ref[...])


def _final(x1, rows2d, w, g):
    t = x1.shape[0]
    row = lambda i: (i, 0)
    return pl.pallas_call(
        _final_body, name="final", grid=(t // TM,),
        in_specs=[pl.BlockSpec((TM, D_MODEL), row), pl.BlockSpec((TM, 2 * D_MODEL), row),
                  pl.BlockSpec((TM, LANES), row), pl.BlockSpec((1, D_MODEL), lambda i: (0, 0))],
        out_specs=pl.BlockSpec((TM, D_MODEL), row),
        out_shape=jax.ShapeDtypeStruct((t, D_MODEL), F32),
        compiler_params=pltpu.CompilerParams(dimension_semantics=("arbitrary",), vmem_limit_bytes=VMEM_LIMIT),
    )(x1, rows2d, w, g)


def _layer(x2d, nb, s, norm_mix_g, w_in, b_in, attn_sinks, w_attn_up, w_ret_up, w_out, norm_ffn_g,
           w_group_router, b_group_router, w_expert_router, b_expert_router, w_gate_e, w_up_e, w_down_e, out_g):
    t = x2d.shape[0]
    w_bf = w_in.astype(BF16)
    wkt = jnp.concatenate([w_in[:, C_AK:C_AV], w_in[:, C_RK:C_RV]], axis=1).T.astype(BF16)
    bkt = jnp.concatenate([b_in[C_AK:C_AV], b_in[C_RK:C_RV]])[:, None]
    aq, akt, av, rq, rkt, rv, rg, ga, gr = _inproj(x2d, norm_mix_g[None, :], w_bf, b_in[None, :], wkt, bkt)
    attn = _attention(attn_sinks, aq, akt, av, nb, s)
    ret = _retention(rq, rkt, rv, rg, nb, s)

    wr = jnp.zeros((D_MODEL, ROUTER_W), F32)
    wr = wr.at[:, 0:N_GROUPS].set(w_group_router).at[:, 8:8 + N_EXPERTS].set(w_expert_router).astype(BF16)
    br = jnp.zeros((1, ROUTER_W), F32)
    br = br.at[0, 0:N_GROUPS].set(b_group_router).at[0, 8:8 + N_EXPERTS].set(b_expert_router)
    x1, logits = _mix(attn, ret, ga, gr, x2d, w_attn_up.astype(BF16), w_ret_up.astype(BF16), w_out.astype(BF16),
                      norm_ffn_g[None, :], wr, br)

    n_asg = 2 * t
    n_slots = n_asg + N_EXPERTS * BLK
    nblocks = n_slots // BLK
    nblkp = -(-nblocks // LANES) * LANES
    info, wts, cnt = _route(logits)
    dest, pst, be, nblk = _dest(info, cnt, nblkp)
    slot = _perm(dest[0:2].reshape(n_asg // LANES, LANES), cnt, pst, t, n_slots)
    rows = _experts(be[0, :nblocks], nblk[0, 0:1], slot.reshape(nblocks, BLK // LANES, LANES),
                    x1.reshape(t * SUB, LANES), norm_ffn_g[None, :], w_gate_e, w_up_e, w_down_e, n_slots)
    return _final(x1, rows.reshape(n_slots // 2, 2 * D_MODEL), wts, out_g)


def kernel(x, norm_mix_g, w_in, b_in, attn_sinks, w_attn_up, w_ret_up, w_out, norm_ffn_g, w_group_router,
           b_group_router, w_expert_router, b_expert_router, w_gate_e, w_up_e, w_down_e, norm_final_g):
    nb, s, d = x.shape
    depth = w_in.shape[0]
    assert d == D_MODEL and s % TQ == 0 and (nb * s) % TM == 0
    x2d = x.reshape(nb * s, d)
    for l in range(depth):
        assert l == depth - 1, "only the last layer may feed the final norm"
        x2d = _layer(x2d, nb, s, norm_mix_g[l], w_in[l], b_in[l], attn_sinks[l], w_attn_up[l], w_ret_up[l], w_out[l],
                     norm_ffn_g[l], w_group_router[l], b_group_router[l], w_expert_router[l], b_expert_router[l],
                     w_gate_e[l], w_up_e[l], w_down_e[l], norm_final_g[None, :])
    return x2d.reshape(nb, s, d)
```
